```python
import jax, jax.numpy as jnp
from jax import lax
import numpy as np

D_MODEL = 1024
BATCH = 32
SEQ = 2048
DEPTH = 2
DEC_BATCH = 8
DEC_SEQ = 16
PAST_LEN = 4096

CHUNK = 64
POOL_WINDOWS = (2, 4, 8, 16)
POOL_PAST = max(POOL_WINDOWS) - 1
POOL_WIDTH = D_MODEL // 4
POOL_GROUP = POOL_WIDTH // len(POOL_WINDOWS)
CONV_WIDTH = 3 * D_MODEL // 8
CONV_K = 31
SCONV_WIDTH = D_MODEL - POOL_WIDTH - CONV_WIDTH
SCONV_K = 3
MIX_WIDTH = POOL_WIDTH + CONV_WIDTH + SCONV_WIDTH
IN_WIDTH = POOL_WIDTH + 2 * CONV_WIDTH + 3 * SCONV_WIDTH
D_FF = -(-8 * D_MODEL // (3 * 256)) * 256
EPS = 1e-6

kernel_name = 'hybrid_pool_conv_streaming_step'


def _rmsnorm(x, g):
    xf = x.astype(jnp.float32)
    y = xf * lax.rsqrt(jnp.mean(xf * xf, axis=-1, keepdims=True) + EPS) * g.astype(jnp.float32)
    return y.astype(x.dtype)


def _depthwise_causal(ext, w):
    C = ext.shape[-1]
    return lax.conv_general_dilated(ext, w.astype(ext.dtype)[:, None, :], (1,), 'VALID',
                                    dimension_numbers=('NWC', 'WIO', 'NWC'),
                                    feature_group_count=C)


def _pool_mixer(xa, past, start, w_pool, s_pool):
    B, T, _ = xa.shape
    ext = jnp.concatenate([past.astype(xa.dtype), xa], axis=1)
    cs = jnp.pad(jnp.cumsum(ext.astype(jnp.float32), axis=1), ((0, 0), (1, 0), (0, 0)))
    pos = jnp.arange(T, dtype=jnp.float32) + start
    means = []
    for g, w in enumerate(POOL_WINDOWS):
        lo, hi = g * POOL_GROUP, (g + 1) * POOL_GROUP
        wsum = (cs[:, POOL_PAST + 1:POOL_PAST + 1 + T, lo:hi]
                - cs[:, POOL_PAST + 1 - w:POOL_PAST + 1 - w + T, lo:hi])
        cnt = jnp.minimum(pos + 1.0, float(w))[None, :, None]
        means.append(wsum / cnt)
    pooled = jnp.concatenate(means, axis=-1) - xa.astype(jnp.float32)
    pooled = pooled.reshape(B, T, len(POOL_WINDOWS), POOL_GROUP)
    mixed = jnp.einsum('btgc,gcd->btgd', pooled, w_pool.astype(jnp.float32)).reshape(B, T, POOL_WIDTH)
    y = (mixed * s_pool.astype(jnp.float32)).astype(xa.dtype)
    return y, ext[:, -POOL_PAST:]


def _conformer_conv(xb, past, w_dw, b_dw, ln_g, ln_b):
    a, g = jnp.split(xb, 2, axis=-1)
    u = a * jax.nn.sigmoid(g)
    ext = jnp.concatenate([past.astype(u.dtype), u], axis=1)
    z = (_depthwise_causal(ext, w_dw) + b_dw.astype(u.dtype)).astype(jnp.float32)
    mu = jnp.mean(z, axis=-1, keepdims=True)
    zc = z - mu
    var = jnp.mean(zc * zc, axis=-1, keepdims=True)
    z = zc * lax.rsqrt(var + EPS) * ln_g.astype(jnp.float32) + ln_b.astype(jnp.float32)
    return jax.nn.silu(z).astype(xb.dtype), ext[:, -(CONV_K - 1):]


def _short_conv(xc, past, w_sc):
    h, bg, cg = jnp.split(xc, 3, axis=-1)
    u = cg * h
    ext = jnp.concatenate([past.astype(u.dtype), u], axis=1)
    y = bg * _depthwise_causal(ext, w_sc)
    return y, ext[:, -(SCONV_K - 1):]


def _layer(x, c, pool_past, conv_past, sconv_past, start,
           w_ada, b_ada, g_pre_mix, g_post_mix, g_pre_ffn, g_post_ffn,
           w_in, w_pool, s_pool, w_dw, b_dw, ln_g, ln_b, w_sc, w_out,
           w_gate, w_up, w_down):
    mod = jax.nn.silu(c) @ w_ada + b_ada
    sh1, sc1, gt1, sh2, sc2, gt2 = [m[:, None, :] for m in jnp.split(mod, 6, axis=-1)]
    h = _rmsnorm(x, g_pre_mix) * (1 + sc1) + sh1
    proj = h @ w_in
    xa = proj[..., :POOL_WIDTH]
    xb = proj[..., POOL_WIDTH:POOL_WIDTH + 2 * CONV_WIDTH]
    xc = proj[..., POOL_WIDTH + 2 * CONV_WIDTH:]
    ya, pool_new = _pool_mixer(xa, pool_past, start, w_pool, s_pool)
    yb, conv_new = _conformer_conv(xb, conv_past, w_dw, b_dw, ln_g, ln_b)
    yc, sconv_new = _short_conv(xc, sconv_past, w_sc)
    mix = jnp.concatenate([ya, yb, yc], axis=-1) @ w_out
    x = x + gt1 * _rmsnorm(mix, g_post_mix)
    h = _rmsnorm(x, g_pre_ffn) * (1 + sc2) + sh2
    f = (jax.nn.silu(h @ w_gate) * (h @ w_up)) @ w_down
    x = x + gt2 * _rmsnorm(f, g_post_ffn)
    return x, pool_new, conv_new, sconv_new


def setup_inputs(seed: int = 0) -> dict:
    key = jax.random.key(seed)
    ks = jax.random.split(key, 25)
    f32 = jnp.float32

    def nrm(k, shape, s):
        return jax.random.normal(k, shape, f32) * s

    D = D_MODEL
    return {
        'x_prompt': nrm(ks[0], (BATCH, SEQ, D), 1.0),
        'x_sample': nrm(ks[1], (DEC_BATCH, DEC_SEQ, D), 1.0),
        'c_prompt': nrm(ks[2], (BATCH, D), 1.0),
        'c_sample': nrm(ks[3], (DEC_BATCH, D), 1.0),
        'state_pool': nrm(ks[4], (DEPTH, DEC_BATCH, POOL_PAST, POOL_WIDTH), 1.0),
        'state_conv': nrm(ks[5], (DEPTH, DEC_BATCH, CONV_K - 1, CONV_WIDTH), 0.5),
        'state_sconv': nrm(ks[6], (DEPTH, DEC_BATCH, SCONV_K - 1, SCONV_WIDTH), 1.0),
        'w_ada': nrm(ks[7], (DEPTH, D, 6 * D), 0.5 * D ** -0.5),
        'b_ada': nrm(ks[8], (DEPTH, 6 * D), 0.02),
        'g_pre_mix': 1.0 + nrm(ks[9], (DEPTH, D), 0.05),
        'g_post_mix': 1.0 + nrm(ks[10], (DEPTH, D), 0.05),
        'g_pre_ffn': 1.0 + nrm(ks[11], (DEPTH, D), 0.05),
        'g_post_ffn': 1.0 + nrm(ks[12], (DEPTH, D), 0.05),
        'w_in': nrm(ks[13], (DEPTH, D, IN_WIDTH), D ** -0.5),
        'w_pool': nrm(ks[14], (DEPTH, len(POOL_WINDOWS), POOL_GROUP, POOL_GROUP), POOL_GROUP ** -0.5),
        's_pool': 1.0 + nrm(ks[15], (DEPTH, POOL_WIDTH), 0.05),
        'w_dw': nrm(ks[16], (DEPTH, CONV_K, CONV_WIDTH), CONV_K ** -0.5),
        'b_dw': nrm(ks[17], (DEPTH, CONV_WIDTH), 0.02),
        'ln_g': 1.0 + nrm(ks[18], (DEPTH, CONV_WIDTH), 0.05),
        'ln_b': nrm(ks[19], (DEPTH, CONV_WIDTH), 0.02),
        'w_sc': nrm(ks[20], (DEPTH, SCONV_K, SCONV_WIDTH), SCONV_K ** -0.5),
        'w_out': nrm(ks[21], (DEPTH, MIX_WIDTH, D), MIX_WIDTH ** -0.5),
        'w_gate': nrm(ks[22], (DEPTH, D, D_FF), D ** -0.5),
        'w_up': nrm(ks[23], (DEPTH, D, D_FF), D ** -0.5),
        'w_down': nrm(ks[24], (DEPTH, D_FF, D), D_FF ** -0.5),
    }


def reference(x_prompt, x_sample, c_prompt, c_sample, state_pool, state_conv, state_sconv,
              w_ada, b_ada, g_pre_mix, g_post_mix, g_pre_ffn, g_post_ffn,
              w_in, w_pool, s_pool, w_dw, b_dw, ln_g, ln_b, w_sc, w_out,
              w_gate, w_up, w_down):
    xp, xs = x_prompt, x_sample
    bp = xp.shape[0]
    pool_p, conv_p, sconv_p = [], [], []
    pool_s, conv_s, sconv_s = [], [], []
    for l in range(DEPTH):
        lp = [w[l] for w in (w_ada, b_ada, g_pre_mix, g_post_mix, g_pre_ffn, g_post_ffn,
                             w_in, w_pool, s_pool, w_dw, b_dw, ln_g, ln_b, w_sc, w_out,
                             w_gate, w_up, w_down)]
        zp = jnp.zeros((bp, POOL_PAST, POOL_WIDTH), xp.dtype)
        zc = jnp.zeros((bp, CONV_K - 1, CONV_WIDTH), xp.dtype)
        zs = jnp.zeros((bp, SCONV_K - 1, SCONV_WIDTH), xp.dtype)
        xp, a, b, c = _layer(xp, c_prompt, zp, zc, zs, 0, *lp)
        pool_p.append(a); conv_p.append(b); sconv_p.append(c)
        xs, a, b, c = _layer(xs, c_sample, state_pool[l], state_conv[l], state_sconv[l], PAST_LEN, *lp)
        pool_s.append(a); conv_s.append(b); sconv_s.append(c)
    return (xp, xs,
            jnp.stack(pool_p), jnp.stack(conv_p), jnp.stack(sconv_p),
            jnp.stack(pool_s), jnp.stack(conv_s), jnp.stack(sconv_s))
```

```python
import functools

import jax
import jax.numpy as jnp
from jax import lax
from jax.experimental import pallas as pl
from jax.experimental.pallas import tpu as pltpu

D_MODEL = 1024
POOL_WINDOWS = (2, 4, 8, 16)
POOL_PAST = max(POOL_WINDOWS) - 1
POOL_WIDTH = D_MODEL // 4
POOL_GROUP = POOL_WIDTH // len(POOL_WINDOWS)
CONV_WIDTH = 3 * D_MODEL // 8
CONV_K = 31
CONV_PAST = CONV_K - 1
SCONV_WIDTH = D_MODEL - POOL_WIDTH - CONV_WIDTH
SCONV_K = 3
SCONV_PAST = SCONV_K - 1
IN_WIDTH = POOL_WIDTH + 2 * CONV_WIDTH + 3 * SCONV_WIDTH
D_FF = 2816
EPS = 1e-6
PAST_LEN = 4096

SUBLANES_V7X = 8
LANES_V7X = 128
MXU_COLS_V7X = 256
VMEM_BYTES_V7X = 64 * 1024 * 1024

_XA0 = 0
_CA0 = POOL_WIDTH
_CG0 = _CA0 + CONV_WIDTH
_SH0 = _CG0 + CONV_WIDTH
_SB0 = _SH0 + SCONV_WIDTH
_SC0 = _SB0 + SCONV_WIDTH

_POOL_HEAD = 16
_CONV_HEAD = 32
_SCONV_HEAD = 8

FF_CHUNK = MXU_COLS_V7X
N_FF_CHUNKS = D_FF // FF_CHUNK
CONV_ROW_CHUNK = 32


def _vmem_limit(nbytes):
    return int(min(nbytes, VMEM_BYTES_V7X - 4 * 1024 * 1024))


def _resident(shape):
    zeros = (0,) * len(shape)
    return pl.BlockSpec(shape, lambda *_: zeros, pipeline_mode=pl.Buffered(1))


def _mod_kernel(c_ref, w_ref, b_ref, o_ref):
    c = c_ref[...]
    s = (c * jax.nn.sigmoid(c)).astype(jnp.bfloat16)
    w = w_ref[0].astype(jnp.bfloat16)
    o_ref[0, 0] = jnp.dot(s, w, preferred_element_type=jnp.float32) + b_ref[0, 0]


def _modulation(c_all, w_ada, b_ada):
    depth = w_ada.shape[0]
    nb = c_all.shape[0]
    b4 = b_ada.reshape(depth, 6, 1, D_MODEL)
    return pl.pallas_call(
        _mod_kernel,
        out_shape=jax.ShapeDtypeStruct((depth, 6, nb, D_MODEL), jnp.float32),
        grid=(depth, 6),
        in_specs=[
            pl.BlockSpec((nb, D_MODEL), lambda l, j: (0, 0)),
            pl.BlockSpec((1, D_MODEL, D_MODEL), lambda l, j: (l, 0, j)),
            pl.BlockSpec((1, 1, 1, D_MODEL), lambda l, j: (l, j, 0, 0)),
        ],
        out_specs=pl.BlockSpec((1, 1, nb, D_MODEL), lambda l, j: (l, j, 0, 0)),
        compiler_params=pltpu.CompilerParams(
            dimension_semantics=("arbitrary", "arbitrary"),
            vmem_limit_bytes=_vmem_limit(32 * 1024 * 1024)),
        name="adaln_mod",
    )(c_all, w_ada, b4)


def _rms_scale(v):
    return lax.rsqrt(jnp.mean(v * v, axis=-1, keepdims=True) + EPS)


def _mix_kernel(x_ref, mod_ref, sp_ref, sc_ref, ss_ref,
                gpre_ref, gpost_ref, win_ref, wpool_ref, spool_ref,
                wdw_ref, bdw_ref, lng_ref, lnb_ref, wsc_ref, wout_ref,
                y_ref, po_ref, co_ref, so_ref,
                pbuf, cbuf, sbuf, mixbuf, *, nb, tm, start):
    t = pl.program_id(1)
    rows = nb * tm

    @pl.when(t == 0)
    def _load_state():
        pbuf[:, 0:_POOL_HEAD, :] = jnp.zeros((nb, _POOL_HEAD, POOL_WIDTH), jnp.float32)
        cbuf[:, 0:_CONV_HEAD, :] = jnp.zeros((nb, _CONV_HEAD, CONV_WIDTH), jnp.float32)
        sbuf[:, 0:_SCONV_HEAD, :] = jnp.zeros((nb, _SCONV_HEAD, SCONV_WIDTH), jnp.float32)
        pbuf[:, _POOL_HEAD - POOL_PAST:_POOL_HEAD, :] = sp_ref[...]
        cbuf[:, _CONV_HEAD - CONV_PAST:_CONV_HEAD, :] = sc_ref[...]
        sbuf[:, _SCONV_HEAD - SCONV_PAST:_SCONV_HEAD, :] = ss_ref[...]

    x = x_ref[...]
    sh1 = mod_ref[0]
    sc1 = mod_ref[1]
    gt1 = mod_ref[2]
    h = (x * _rms_scale(x)) * (gpre_ref[...] * (1.0 + sc1)) + sh1
    hb = h.reshape(rows, D_MODEL).astype(jnp.bfloat16)

    def proj(c0, width):
        return jnp.dot(hb, win_ref[:, c0:c0 + width], preferred_element_type=jnp.float32)

    xa = proj(_XA0, POOL_WIDTH).reshape(nb, tm, POOL_WIDTH)
    pbuf[:, _POOL_HEAD:_POOL_HEAD + tm, :] = xa

    def window_sum(lane0, width, w):
        acc = xa[:, :, lane0:lane0 + width]
        for j in range(1, w):
            acc = acc + pbuf[:, _POOL_HEAD - j:_POOL_HEAD - j + tm, lane0:lane0 + width]
        return acc

    half = 2 * POOL_GROUP
    lane = lax.broadcasted_iota(jnp.int32, (1, 1, half), 2)
    first = lane < POOL_GROUP
    ws_lo = jnp.where(first, window_sum(0, half, POOL_WINDOWS[0]), window_sum(0, half, POOL_WINDOWS[1]))
    ws_hi = jnp.where(first, window_sum(half, half, POOL_WINDOWS[2]), window_sum(half, half, POOL_WINDOWS[3]))
    wlen_lo = jnp.where(first, float(POOL_WINDOWS[0]), float(POOL_WINDOWS[1]))
    wlen_hi = jnp.where(first, float(POOL_WINDOWS[2]), float(POOL_WINDOWS[3]))
    pos1 = (lax.broadcasted_iota(jnp.int32, (1, tm, half), 1) + (t * tm + start + 1)).astype(jnp.float32)
    pooled_lo = ws_lo / jnp.minimum(pos1, wlen_lo) - xa[:, :, 0:half]
    pooled_hi = ws_hi / jnp.minimum(pos1, wlen_hi) - xa[:, :, half:POOL_WIDTH]
    pooled = jnp.concatenate([pooled_lo, pooled_hi], axis=-1).reshape(rows, POOL_WIDTH)
    mixed = jnp.dot(pooled.astype(jnp.bfloat16), wpool_ref[...], preferred_element_type=jnp.float32)
    mixbuf[:, 0:POOL_WIDTH] = (mixed * spool_ref[...]).astype(jnp.bfloat16)

    u = proj(_CA0, CONV_WIDTH) * jax.nn.sigmoid(proj(_CG0, CONV_WIDTH))
    cbuf[:, _CONV_HEAD:_CONV_HEAD + tm, :] = u.reshape(nb, tm, CONV_WIDTH)
    rc = min(CONV_ROW_CHUNK, tm)
    base = _CONV_HEAD - CONV_PAST
    for r0 in range(0, tm, rc):
        z = cbuf[:, base + r0:base + r0 + rc, :] * wdw_ref[0:1, :]
        for k in range(1, CONV_K):
            z = z + cbuf[:, base + r0 + k:base + r0 + k + rc, :] * wdw_ref[k:k + 1, :]
        z = z + bdw_ref[...]
        mu = jnp.mean(z, axis=-1, keepdims=True)
        zc = z - mu
        var = jnp.mean(zc * zc, axis=-1, keepdims=True)
        zn = zc * lax.rsqrt(var + EPS) * lng_ref[...] + lnb_ref[...]
        yb = zn * jax.nn.sigmoid(zn)
        for b in range(nb):
            mixbuf[b * tm + r0:b * tm + r0 + rc, POOL_WIDTH:POOL_WIDTH + CONV_WIDTH] = yb[b].astype(jnp.bfloat16)

    us = (proj(_SC0, SCONV_WIDTH) * proj(_SH0, SCONV_WIDTH)).reshape(nb, tm, SCONV_WIDTH)
    sbuf[:, _SCONV_HEAD:_SCONV_HEAD + tm, :] = us
    sbase = _SCONV_HEAD - SCONV_PAST
    cv = us * wsc_ref[SCONV_PAST:SCONV_K, :]
    for k in range(SCONV_PAST):
        cv = cv + sbuf[:, sbase + k:sbase + k + tm, :] * wsc_ref[k:k + 1, :]
    yc = proj(_SB0, SCONV_WIDTH) * cv.reshape(rows, SCONV_WIDTH)
    mixbuf[:, POOL_WIDTH + CONV_WIDTH:D_MODEL] = yc.astype(jnp.bfloat16)

    m = jnp.dot(mixbuf[...], wout_ref[...], preferred_element_type=jnp.float32).reshape(nb, tm, D_MODEL)
    y_ref[...] = x + (m * _rms_scale(m)) * (gpost_ref[...] * gt1)

    po_ref[...] = pbuf[:, _POOL_HEAD + tm - POOL_PAST:_POOL_HEAD + tm, :]
    co_ref[...] = cbuf[:, _CONV_HEAD + tm - CONV_PAST:_CONV_HEAD + tm, :]
    so_ref[...] = sbuf[:, _SCONV_HEAD + tm - SCONV_PAST:_SCONV_HEAD + tm, :]
    pbuf[:, 0:_POOL_HEAD, :] = pbuf[:, tm:tm + _POOL_HEAD, :]
    cbuf[:, 0:_CONV_HEAD, :] = cbuf[:, tm:tm + _CONV_HEAD, :]
    sbuf[:, 0:_SCONV_HEAD, :] = sbuf[:, tm:tm + _SCONV_HEAD, :]


def _mix_call(x, mod, sp, sc, ss, gpre, gpost, win, wpool, spool, wdw, bdw, lng, lnb, wsc, wout,
              *, nb, tm, start):
    bsz, seq, _ = x.shape
    grid = (bsz // nb, seq // tm)
    kern = functools.partial(_mix_kernel, nb=nb, tm=tm, start=start)
    row = lambda width: _resident((1, width))
    state_spec = lambda past, width: pl.BlockSpec((nb, past, width), lambda b, t: (b, 0, 0))
    out_shapes = (
        jax.ShapeDtypeStruct(x.shape, jnp.float32),
        jax.ShapeDtypeStruct((bsz, POOL_PAST, POOL_WIDTH), jnp.float32),
        jax.ShapeDtypeStruct((bsz, CONV_PAST, CONV_WIDTH), jnp.float32),
        jax.ShapeDtypeStruct((bsz, SCONV_PAST, SCONV_WIDTH), jnp.float32),
    )
    return pl.pallas_call(
        kern,
        out_shape=out_shapes,
        grid=grid,
        in_specs=[
            pl.BlockSpec((nb, tm, D_MODEL), lambda b, t: (b, t, 0)),
            pl.BlockSpec((3, nb, 1, D_MODEL), lambda b, t: (0, b, 0, 0)),
            state_spec(POOL_PAST, POOL_WIDTH),
            state_spec(CONV_PAST, CONV_WIDTH),
            state_spec(SCONV_PAST, SCONV_WIDTH),
            row(D_MODEL), row(D_MODEL),
            _resident((D_MODEL, IN_WIDTH)),
            _resident((POOL_WIDTH, POOL_WIDTH)),
            row(POOL_WIDTH),
            _resident((CONV_K, CONV_WIDTH)),
            row(CONV_WIDTH), row(CONV_WIDTH), row(CONV_WIDTH),
            _resident((SCONV_K, SCONV_WIDTH)),
            _resident((D_MODEL, D_MODEL)),
        ],
        out_specs=(
            pl.BlockSpec((nb, tm, D_MODEL), lambda b, t: (b, t, 0)),
            state_spec(POOL_PAST, POOL_WIDTH),
            state_spec(CONV_PAST, CONV_WIDTH),
            state_spec(SCONV_PAST, SCONV_WIDTH),
        ),
        scratch_shapes=[
            pltpu.VMEM((nb, _POOL_HEAD + tm, POOL_WIDTH), jnp.float32),
            pltpu.VMEM((nb, _CONV_HEAD + tm, CONV_WIDTH), jnp.float32),
            pltpu.VMEM((nb, _SCONV_HEAD + tm, SCONV_WIDTH), jnp.float32),
            pltpu.VMEM((nb * tm, D_MODEL), jnp.bfloat16),
        ],
        compiler_params=pltpu.CompilerParams(
            dimension_semantics=("parallel", "arbitrary"),
            vmem_limit_bytes=_vmem_limit(48 * 1024 * 1024)),
        name="token_mix",
    )(x, mod, sp, sc, ss, gpre, gpost, win, wpool, spool, wdw, bdw, lng, lnb, wsc, wout)


def _ffn_kernel(x_ref, mod_ref, gpre_ref, gpost_ref, wgu_ref, wd_ref, y_ref, *, nb, tm):
    rows = nb * tm
    x = x_ref[...]
    sh2 = mod_ref[0]
    sc2 = mod_ref[1]
    gt2 = mod_ref[2]
    h = (x * _rms_scale(x)) * (gpre_ref[...] * (1.0 + sc2)) + sh2
    hb = h.reshape(rows, D_MODEL).astype(jnp.bfloat16)
    f = None
    for j in range(N_FF_CHUNKS):
        gu = jnp.dot(hb, wgu_ref[j], preferred_element_type=jnp.float32)
        g = gu[:, 0:FF_CHUNK]
        act = (g * jax.nn.sigmoid(g) * gu[:, FF_CHUNK:2 * FF_CHUNK]).astype(jnp.bfloat16)
        part = jnp.dot(act, wd_ref[j], preferred_element_type=jnp.float32)
        f = part if f is None else f + part
    f = f.reshape(nb, tm, D_MODEL)
    y_ref[...] = x + (f * _rms_scale(f)) * (gpost_ref[...] * gt2)


def _ffn_call(x, mod, gpre, gpost, wgu, wd, *, nb, tm):
    bsz, seq, _ = x.shape
    kern = functools.partial(_ffn_kernel, nb=nb, tm=tm)
    return pl.pallas_call(
        kern,
        out_shape=jax.ShapeDtypeStruct(x.shape, jnp.float32),
        grid=(bsz // nb, seq // tm),
        in_specs=[
            pl.BlockSpec((nb, tm, D_MODEL), lambda b, t: (b, t, 0)),
            pl.BlockSpec((3, nb, 1, D_MODEL), lambda b, t: (0, b, 0, 0)),
            _resident((1, D_MODEL)), _resident((1, D_MODEL)),
            _resident((N_FF_CHUNKS, D_MODEL, 2 * FF_CHUNK)),
            _resident((N_FF_CHUNKS, FF_CHUNK, D_MODEL)),
        ],
        out_specs=pl.BlockSpec((nb, tm, D_MODEL), lambda b, t: (b, t, 0)),
        compiler_params=pltpu.CompilerParams(
            dimension_semantics=("parallel", "arbitrary"),
            vmem_limit_bytes=_vmem_limit(56 * 1024 * 1024)),
        name="channel_mix",
    )(x, mod, gpre, gpost, wgu, wd)


def _block_diag(w_pool_l):
    g = w_pool_l.shape[0]
    eye = jnp.eye(g, dtype=w_pool_l.dtype)
    return jnp.einsum("gcd,gh->gchd", w_pool_l, eye).reshape(g * POOL_GROUP, g * POOL_GROUP)


def kernel(x_prompt, x_sample, c_prompt, c_sample, state_pool, state_conv, state_sconv,
           w_ada, b_ada, g_pre_mix, g_post_mix, g_pre_ffn, g_post_ffn,
           w_in, w_pool, s_pool, w_dw, b_dw, ln_g, ln_b, w_sc, w_out,
           w_gate, w_up, w_down):
    depth = w_ada.shape[0]
    bp, seq_p, _ = x_prompt.shape
    bs, seq_s, _ = x_sample.shape
    bf16 = jnp.bfloat16

    mod = _modulation(jnp.concatenate([c_prompt, c_sample], axis=0), w_ada, b_ada)
    mod = mod[:, :, :, None, :]

    tm_p = 512
    zero_p = jnp.zeros((bp, POOL_PAST, POOL_WIDTH), jnp.float32)
    zero_c = jnp.zeros((bp, CONV_PAST, CONV_WIDTH), jnp.float32)
    zero_s = jnp.zeros((bp, SCONV_PAST, SCONV_WIDTH), jnp.float32)

    xp, xs = x_prompt, x_sample
    outs_p = ([], [], [])
    outs_s = ([], [], [])
    for l in range(depth):
        win = w_in[l].astype(bf16)
        wpool = _block_diag(w_pool[l]).astype(bf16)
        wout = w_out[l].astype(bf16)
        wgu = jnp.concatenate(
            [w_gate[l].reshape(D_MODEL, N_FF_CHUNKS, FF_CHUNK),
             w_up[l].reshape(D_MODEL, N_FF_CHUNKS, FF_CHUNK)], axis=-1
        ).transpose(1, 0, 2).astype(bf16)
        wd = w_down[l].reshape(N_FF_CHUNKS, FF_CHUNK, D_MODEL).astype(bf16)
        row = lambda a: a[l][None, :]
        mix_w = (row(g_pre_mix), row(g_post_mix), win, wpool, row(s_pool), w_dw[l], row(b_dw),
                 row(ln_g), row(ln_b), w_sc[l], wout)
        ffn_w = (row(g_pre_ffn), row(g_post_ffn), wgu, wd)
        mod_p = mod[l, :, :bp]
        mod_s = mod[l, :, bp:]

        xp, a, b, c = _mix_call(xp, mod_p[0:3], zero_p, zero_c, zero_s, *mix_w,
                                nb=1, tm=tm_p, start=0)
        for lst, v in zip(outs_p, (a, b, c)):
            lst.append(v)
        xp = _ffn_call(xp, mod_p[3:6], *ffn_w, nb=1, tm=tm_p)

        xs, a, b, c = _mix_call(xs, mod_s[0:3], state_pool[l], state_conv[l], state_sconv[l], *mix_w,
                                nb=bs, tm=seq_s, start=PAST_LEN)
        for lst, v in zip(outs_s, (a, b, c)):
            lst.append(v)
        xs = _ffn_call(xs, mod_s[3:6], *ffn_w, nb=bs, tm=seq_s)

    return (xp, xs,
            jnp.stack(outs_p[0]), jnp.stack(outs_p[1]), jnp.stack(outs_p[2]),
            jnp.stack(outs_s[0]), jnp.stack(outs_s[1]), jnp.stack(outs_s[2]))
```

```python
import functools

import jax
import jax.numpy as jnp
from jax import lax
from jax.experimental import pallas as pl
from jax.experimental.pallas import tpu as pltpu

D_MODEL = 1024
POOL_WINDOWS = (2, 4, 8, 16)
POOL_PAST = max(POOL_WINDOWS) - 1
POOL_WIDTH = D_MODEL // 4
POOL_GROUP = POOL_WIDTH // len(POOL_WINDOWS)
CONV_WIDTH = 3 * D_MODEL // 8
CONV_K = 31
CONV_PAST = CONV_K - 1
SCONV_WIDTH = D_MODEL - POOL_WIDTH - CONV_WIDTH
SCONV_K = 3
SCONV_PAST = SCONV_K - 1
IN_WIDTH = POOL_WIDTH + 2 * CONV_WIDTH + 3 * SCONV_WIDTH
D_FF = 2816
EPS = 1e-6
PAST_LEN = 4096

SUBLANES_V7X = 8
LANES_V7X = 128
MXU_COLS_V7X = 256
VMEM_BYTES_V7X = 64 * 1024 * 1024

_XA0 = 0
_CA0 = POOL_WIDTH
_CG0 = _CA0 + CONV_WIDTH
_SH0 = _CG0 + CONV_WIDTH
_SB0 = _SH0 + SCONV_WIDTH
_SC0 = _SB0 + SCONV_WIDTH

_POOL_HEAD = 16
_CONV_HEAD = 32
_SCONV_HEAD = 8

_POOL_SLABS = POOL_WIDTH // LANES_V7X
_CONV_SLABS = CONV_WIDTH // LANES_V7X
_ST_POOL = 0
_ST_CONV = _ST_POOL + _POOL_SLABS
_ST_SCONV = _ST_CONV + _CONV_SLABS
_ST_SLABS = _ST_SCONV + _CONV_SLABS
_CT_BIAS = CONV_K
_CT_LNG = CONV_K + 1
_CT_LNB = CONV_K + 2
_CT_ROWS = CONV_K + 3
_MIX_BLOCK_ROWS = 4 * SUBLANES_V7X

FF_CHUNK = MXU_COLS_V7X
N_FF_CHUNKS = D_FF // FF_CHUNK


def _vmem_limit(nbytes):
    return int(min(nbytes, VMEM_BYTES_V7X - 4 * 1024 * 1024))


def _resident(shape):
    zeros = (0,) * len(shape)
    return pl.BlockSpec(shape, lambda *_: zeros, pipeline_mode=pl.Buffered(1))


def _mod_kernel(c_ref, w_ref, b_ref, o_ref):
    c = c_ref[...]
    s = (c * jax.nn.sigmoid(c)).astype(jnp.bfloat16)
    w = w_ref[0].astype(jnp.bfloat16)
    o_ref[0, 0] = jnp.dot(s, w, preferred_element_type=jnp.float32) + b_ref[0, 0]


def _modulation(c_all, w_ada, b_ada):
    depth = w_ada.shape[0]
    nb = c_all.shape[0]
    b4 = b_ada.reshape(depth, 6, 1, D_MODEL)
    return pl.pallas_call(
        _mod_kernel,
        out_shape=jax.ShapeDtypeStruct((depth, 6, nb, D_MODEL), jnp.float32),
        grid=(depth, 6),
        in_specs=[
            pl.BlockSpec((nb, D_MODEL), lambda l, j: (0, 0)),
            pl.BlockSpec((1, D_MODEL, D_MODEL), lambda l, j: (l, 0, j)),
            pl.BlockSpec((1, 1, 1, D_MODEL), lambda l, j: (l, j, 0, 0)),
        ],
        out_specs=pl.BlockSpec((1, 1, nb, D_MODEL), lambda l, j: (l, j, 0, 0)),
        compiler_params=pltpu.CompilerParams(
            dimension_semantics=("arbitrary", "arbitrary"),
            vmem_limit_bytes=_vmem_limit(32 * 1024 * 1024)),
        name="adaln_mod",
    )(c_all, w_ada, b4)


def _rms_scale(v):
    return lax.rsqrt(jnp.mean(v * v, axis=-1, keepdims=True) + EPS)


def _lanes(c):
    return slice(c * LANES_V7X, (c + 1) * LANES_V7X)


def _mix_kernel(x_ref, mod_ref, sp_ref, sc_ref, ss_ref,
                gpre_ref, gpost_ref, win_ref, wpool_ref, spool_ref,
                wdw_ref, bdw_ref, lng_ref, lnb_ref, wsc_ref, wout_ref,
                y_ref, po_ref, co_ref, so_ref,
                pbuf, cbuf, sbuf, stage, ctab, stab, mixbuf, *, nb, tm, start):
    t = pl.program_id(1)
    rows = nb * tm
    br = min(_MIX_BLOCK_ROWS, tm)
    rs = br // SUBLANES_V7X
    f32 = jnp.float32

    @pl.when(t == 0)
    def _load_state():
        pbuf[:, :, 0:_POOL_HEAD, :] = jnp.zeros((_POOL_SLABS, nb, _POOL_HEAD, LANES_V7X), f32)
        cbuf[:, :, 0:_CONV_HEAD, :] = jnp.zeros((_CONV_SLABS, nb, _CONV_HEAD, LANES_V7X), f32)
        sbuf[:, :, 0:_SCONV_HEAD, :] = jnp.zeros((_CONV_SLABS, nb, _SCONV_HEAD, LANES_V7X), f32)
        for c in range(_POOL_SLABS):
            pbuf[c, :, _POOL_HEAD - POOL_PAST:_POOL_HEAD, :] = sp_ref[:, :, _lanes(c)]
        for c in range(_CONV_SLABS):
            cbuf[c, :, _CONV_HEAD - CONV_PAST:_CONV_HEAD, :] = sc_ref[:, :, _lanes(c)]
            sbuf[c, :, _SCONV_HEAD - SCONV_PAST:_SCONV_HEAD, :] = ss_ref[:, :, _lanes(c)]
        for k in range(CONV_K):
            ctab[k] = jnp.broadcast_to(wdw_ref[k:k + 1, :], (SUBLANES_V7X, CONV_WIDTH))
        ctab[_CT_BIAS] = jnp.broadcast_to(bdw_ref[...], (SUBLANES_V7X, CONV_WIDTH))
        ctab[_CT_LNG] = jnp.broadcast_to(lng_ref[...], (SUBLANES_V7X, CONV_WIDTH))
        ctab[_CT_LNB] = jnp.broadcast_to(lnb_ref[...], (SUBLANES_V7X, CONV_WIDTH))
        for k in range(SCONV_K):
            stab[k] = jnp.broadcast_to(wsc_ref[k:k + 1, :], (SUBLANES_V7X, SCONV_WIDTH))

    x = x_ref[...]
    sh1 = mod_ref[0]
    sc1 = mod_ref[1]
    gt1 = mod_ref[2]
    h = (x * _rms_scale(x)) * (gpre_ref[...] * (1.0 + sc1)) + sh1
    hb = h.reshape(rows, D_MODEL).astype(jnp.bfloat16)

    def proj(c0, width):
        return jnp.dot(hb, win_ref[:, c0:c0 + width], preferred_element_type=f32)

    def scatter_slabs(buf, head, v, nslabs):
        for c in range(nslabs):
            for b in range(nb):
                buf[c, b, head:head + tm, :] = v[b * tm:(b + 1) * tm, _lanes(c)]

    scatter_slabs(pbuf, _POOL_HEAD, proj(_XA0, POOL_WIDTH), _POOL_SLABS)
    scatter_slabs(cbuf, _CONV_HEAD, proj(_CA0, CONV_WIDTH) * jax.nn.sigmoid(proj(_CG0, CONV_WIDTH)),
                  _CONV_SLABS)
    scatter_slabs(sbuf, _SCONV_HEAD, proj(_SC0, SCONV_WIDTH) * proj(_SH0, SCONV_WIDTH), _CONV_SLABS)

    sub = lax.broadcasted_iota(jnp.int32, (SUBLANES_V7X, LANES_V7X), 0) * rs
    first = lax.broadcasted_iota(jnp.int32, (SUBLANES_V7X, LANES_V7X), 1) < POOL_GROUP
    wlen = [jnp.where(first, float(POOL_WINDOWS[2 * c]), float(POOL_WINDOWS[2 * c + 1]))
            for c in range(_POOL_SLABS)]
    inv_cw = 1.0 / CONV_WIDTH

    def strided(buf, c, b, row):
        return buf[c, b, pl.ds(row, SUBLANES_V7X, stride=rs), :]

    for b in range(nb):
        for r0 in range(0, tm, br):
            def put(slab, j, v):
                stage[slab, pl.ds(b * tm + r0 + j, SUBLANES_V7X, stride=rs), :] = v

            for j in range(rs):
                pos1 = (sub + (t * tm + (r0 + j + start + 1))).astype(f32)
                for c in range(_POOL_SLABS):
                    w_short, w_long = POOL_WINDOWS[2 * c], POOL_WINDOWS[2 * c + 1]
                    cur = strided(pbuf, c, b, _POOL_HEAD + r0 + j)
                    s_short = cur
                    for i in range(1, w_short):
                        s_short = s_short + strided(pbuf, c, b, _POOL_HEAD + r0 + j - i)
                    s_long = s_short
                    for i in range(w_short, w_long):
                        s_long = s_long + strided(pbuf, c, b, _POOL_HEAD + r0 + j - i)
                    wsum = jnp.where(first, s_short, s_long)
                    put(_ST_POOL + c, j, wsum / jnp.minimum(pos1, wlen[c]) - cur)

            z = [[None] * rs for _ in range(_CONV_SLABS)]
            for c in range(_CONV_SLABS):
                taps = [strided(cbuf, c, b, _CONV_HEAD - CONV_PAST + r0 + m)
                        for m in range(CONV_K + rs - 1)]
                for j in range(rs):
                    acc = taps[j] * ctab[0, :, _lanes(c)]
                    for k in range(1, CONV_K):
                        acc = acc + taps[j + k] * ctab[k, :, _lanes(c)]
                    z[c][j] = acc + ctab[_CT_BIAS, :, _lanes(c)]
            for j in range(rs):
                tot = z[0][j]
                for c in range(1, _CONV_SLABS):
                    tot = tot + z[c][j]
                mu = jnp.sum(tot, axis=-1, keepdims=True) * inv_cw
                zc = [z[c][j] - mu for c in range(_CONV_SLABS)]
                sq = zc[0] * zc[0]
                for c in range(1, _CONV_SLABS):
                    sq = sq + zc[c] * zc[c]
                rstd = lax.rsqrt(jnp.sum(sq, axis=-1, keepdims=True) * inv_cw + EPS)
                for c in range(_CONV_SLABS):
                    zn = zc[c] * rstd * ctab[_CT_LNG, :, _lanes(c)] + ctab[_CT_LNB, :, _lanes(c)]
                    put(_ST_CONV + c, j, zn * jax.nn.sigmoid(zn))

            for c in range(_CONV_SLABS):
                taps = [strided(sbuf, c, b, _SCONV_HEAD - SCONV_PAST + r0 + m)
                        for m in range(SCONV_K + rs - 1)]
                for j in range(rs):
                    acc = taps[j] * stab[0, :, _lanes(c)]
                    for k in range(1, SCONV_K):
                        acc = acc + taps[j + k] * stab[k, :, _lanes(c)]
                    put(_ST_SCONV + c, j, acc)

    pooled = jnp.concatenate([stage[_ST_POOL + c] for c in range(_POOL_SLABS)], axis=-1)
    mixed = jnp.dot(pooled.astype(jnp.bfloat16), wpool_ref[...], preferred_element_type=f32)
    mixbuf[:, 0:POOL_WIDTH] = (mixed * spool_ref[...]).astype(jnp.bfloat16)
    for c in range(_CONV_SLABS):
        c0 = POOL_WIDTH + c * LANES_V7X
        mixbuf[:, c0:c0 + LANES_V7X] = stage[_ST_CONV + c].astype(jnp.bfloat16)
    conv3 = jnp.concatenate([stage[_ST_SCONV + c] for c in range(_CONV_SLABS)], axis=-1)
    mixbuf[:, POOL_WIDTH + CONV_WIDTH:D_MODEL] = (proj(_SB0, SCONV_WIDTH) * conv3).astype(jnp.bfloat16)

    m = jnp.dot(mixbuf[...], wout_ref[...], preferred_element_type=f32).reshape(nb, tm, D_MODEL)
    y_ref[...] = x + (m * _rms_scale(m)) * (gpost_ref[...] * gt1)

    for c in range(_POOL_SLABS):
        po_ref[:, :, _lanes(c)] = pbuf[c, :, _POOL_HEAD + tm - POOL_PAST:_POOL_HEAD + tm, :]
    for c in range(_CONV_SLABS):
        co_ref[:, :, _lanes(c)] = cbuf[c, :, _CONV_HEAD + tm - CONV_PAST:_CONV_HEAD + tm, :]
        so_ref[:, :, _lanes(c)] = sbuf[c, :, _SCONV_HEAD + tm - SCONV_PAST:_SCONV_HEAD + tm, :]
    pbuf[:, :, 0:_POOL_HEAD, :] = pbuf[:, :, tm:tm + _POOL_HEAD, :]
    cbuf[:, :, 0:_CONV_HEAD, :] = cbuf[:, :, tm:tm + _CONV_HEAD, :]
    sbuf[:, :, 0:_SCONV_HEAD, :] = sbuf[:, :, tm:tm + _SCONV_HEAD, :]


def _mix_call(x, mod, sp, sc, ss, gpre, gpost, win, wpool, spool, wdw, bdw, lng, lnb, wsc, wout,
              *, nb, tm, start):
    bsz, seq, _ = x.shape
    grid = (bsz // nb, seq // tm)
    kern = functools.partial(_mix_kernel, nb=nb, tm=tm, start=start)
    row = lambda width: _resident((1, width))
    state_spec = lambda past, width: pl.BlockSpec((nb, past, width), lambda b, t: (b, 0, 0))
    out_shapes = (
        jax.ShapeDtypeStruct(x.shape, jnp.float32),
        jax.ShapeDtypeStruct((bsz, POOL_PAST, POOL_WIDTH), jnp.float32),
        jax.ShapeDtypeStruct((bsz, CONV_PAST, CONV_WIDTH), jnp.float32),
        jax.ShapeDtypeStruct((bsz, SCONV_PAST, SCONV_WIDTH), jnp.float32),
    )
    return pl.pallas_call(
        kern,
        out_shape=out_shapes,
        grid=grid,
        in_specs=[
            pl.BlockSpec((nb, tm, D_MODEL), lambda b, t: (b, t, 0)),
            pl.BlockSpec((3, nb, 1, D_MODEL), lambda b, t: (0, b, 0, 0)),
            state_spec(POOL_PAST, POOL_WIDTH),
            state_spec(CONV_PAST, CONV_WIDTH),
            state_spec(SCONV_PAST, SCONV_WIDTH),
            row(D_MODEL), row(D_MODEL),
            _resident((D_MODEL, IN_WIDTH)),
            _resident((POOL_WIDTH, POOL_WIDTH)),
            row(POOL_WIDTH),
            _resident((CONV_K, CONV_WIDTH)),
            row(CONV_WIDTH), row(CONV_WIDTH), row(CONV_WIDTH),
            _resident((SCONV_K, SCONV_WIDTH)),
            _resident((D_MODEL, D_MODEL)),
        ],
        out_specs=(
            pl.BlockSpec((nb, tm, D_MODEL), lambda b, t: (b, t, 0)),
            state_spec(POOL_PAST, POOL_WIDTH),
            state_spec(CONV_PAST, CONV_WIDTH),
            state_spec(SCONV_PAST, SCONV_WIDTH),
        ),
        scratch_shapes=[
            pltpu.VMEM((_POOL_SLABS, nb, _POOL_HEAD + tm, LANES_V7X), jnp.float32),
            pltpu.VMEM((_CONV_SLABS, nb, _CONV_HEAD + tm, LANES_V7X), jnp.float32),
            pltpu.VMEM((_CONV_SLABS, nb, _SCONV_HEAD + tm, LANES_V7X), jnp.float32),
            pltpu.VMEM((_ST_SLABS, nb * tm, LANES_V7X), jnp.float32),
            pltpu.VMEM((_CT_ROWS, SUBLANES_V7X, CONV_WIDTH), jnp.float32),
            pltpu.VMEM((SCONV_K, SUBLANES_V7X, SCONV_WIDTH), jnp.float32),
            pltpu.VMEM((nb * tm, D_MODEL), jnp.bfloat16),
        ],
        compiler_params=pltpu.CompilerParams(
            dimension_semantics=("parallel", "arbitrary"),
            vmem_limit_bytes=_vmem_limit(48 * 1024 * 1024)),
        name="token_mix",
    )(x, mod, sp, sc, ss, gpre, gpost, win, wpool, spool, wdw, bdw, lng, lnb, wsc, wout)


def _ffn_kernel(x_ref, mod_ref, gpre_ref, gpost_ref, wgu_ref, wd_ref, y_ref, *, nb, tm):
    rows = nb * tm
    x = x_ref[...]
    sh2 = mod_ref[0]
    sc2 = mod_ref[1]
    gt2 = mod_ref[2]
    h = (x * _rms_scale(x)) * (gpre_ref[...] * (1.0 + sc2)) + sh2
    hb = h.reshape(rows, D_MODEL).astype(jnp.bfloat16)
    f = None
    for j in range(N_FF_CHUNKS):
        gu = jnp.dot(hb, wgu_ref[j], preferred_element_type=jnp.float32)
        g = gu[:, 0:FF_CHUNK]
        act = (g * jax.nn.sigmoid(g) * gu[:, FF_CHUNK:2 * FF_CHUNK]).astype(jnp.bfloat16)
        part = jnp.dot(act, wd_ref[j], preferred_element_type=jnp.float32)
        f = part if f is None else f + part
    f = f.reshape(nb, tm, D_MODEL)
    y_ref[...] = x + (f * _rms_scale(f)) * (gpost_ref[...] * gt2)


def _ffn_call(x, mod, gpre, gpost, wgu, wd, *, nb, tm):
    bsz, seq, _ = x.shape
    kern = functools.partial(_ffn_kernel, nb=nb, tm=tm)
    return pl.pallas_call(
        kern,
        out_shape=jax.ShapeDtypeStruct(x.shape, jnp.float32),
        grid=(bsz // nb, seq // tm),
        in_specs=[
            pl.BlockSpec((nb, tm, D_MODEL), lambda b, t: (b, t, 0)),
            pl.BlockSpec((3, nb, 1, D_MODEL), lambda b, t: (0, b, 0, 0)),
            _resident((1, D_MODEL)), _resident((1, D_MODEL)),
            _resident((N_FF_CHUNKS, D_MODEL, 2 * FF_CHUNK)),
            _resident((N_FF_CHUNKS, FF_CHUNK, D_MODEL)),
        ],
        out_specs=pl.BlockSpec((nb, tm, D_MODEL), lambda b, t: (b, t, 0)),
        compiler_params=pltpu.CompilerParams(
            dimension_semantics=("parallel", "arbitrary"),
            vmem_limit_bytes=_vmem_limit(56 * 1024 * 1024)),
        name="channel_mix",
    )(x, mod, gpre, gpost, wgu, wd)


def _block_diag(w_pool_l):
    g = w_pool_l.shape[0]
    eye = jnp.eye(g, dtype=w_pool_l.dtype)
    return jnp.einsum("gcd,gh->gchd", w_pool_l, eye).reshape(g * POOL_GROUP, g * POOL_GROUP)


def kernel(x_prompt, x_sample, c_prompt, c_sample, state_pool, state_conv, state_sconv,
           w_ada, b_ada, g_pre_mix, g_post_mix, g_pre_ffn, g_post_ffn,
           w_in, w_pool, s_pool, w_dw, b_dw, ln_g, ln_b, w_sc, w_out,
           w_gate, w_up, w_down):
    depth = w_ada.shape[0]
    bp, seq_p, _ = x_prompt.shape
    bs, seq_s, _ = x_sample.shape
    bf16 = jnp.bfloat16

    mod = _modulation(jnp.concatenate([c_prompt, c_sample], axis=0), w_ada, b_ada)
    mod = mod[:, :, :, None, :]

    tm_p = 512
    zero_p = jnp.zeros((bp, POOL_PAST, POOL_WIDTH), jnp.float32)
    zero_c = jnp.zeros((bp, CONV_PAST, CONV_WIDTH), jnp.float32)
    zero_s = jnp.zeros((bp, SCONV_PAST, SCONV_WIDTH), jnp.float32)

    xp, xs = x_prompt, x_sample
    outs_p = ([], [], [])
    outs_s = ([], [], [])
    for l in range(depth):
        win = w_in[l].astype(bf16)
        wpool = _block_diag(w_pool[l]).astype(bf16)
        wout = w_out[l].astype(bf16)
        wgu = jnp.concatenate(
            [w_gate[l].reshape(D_MODEL, N_FF_CHUNKS, FF_CHUNK),
             w_up[l].reshape(D_MODEL, N_FF_CHUNKS, FF_CHUNK)], axis=-1
        ).transpose(1, 0, 2).astype(bf16)
        wd = w_down[l].reshape(N_FF_CHUNKS, FF_CHUNK, D_MODEL).astype(bf16)
        row = lambda a: a[l][None, :]
        mix_w = (row(g_pre_mix), row(g_post_mix), win, wpool, row(s_pool), w_dw[l], row(b_dw),
                 row(ln_g), row(ln_b), w_sc[l], wout)
        ffn_w = (row(g_pre_ffn), row(g_post_ffn), wgu, wd)
        mod_p = mod[l, :, :bp]
        mod_s = mod[l, :, bp:]

        xp, a, b, c = _mix_call(xp, mod_p[0:3], zero_p, zero_c, zero_s, *mix_w,
                                nb=1, tm=tm_p, start=0)
        for lst, v in zip(outs_p, (a, b, c)):
            lst.append(v)
        xp = _ffn_call(xp, mod_p[3:6], *ffn_w, nb=1, tm=tm_p)

        xs, a, b, c = _mix_call(xs, mod_s[0:3], state_pool[l], state_conv[l], state_sconv[l], *mix_w,
                                nb=bs, tm=seq_s, start=PAST_LEN)
        for lst, v in zip(outs_s, (a, b, c)):
            lst.append(v)
        xs = _ffn_call(xs, mod_s[3:6], *ffn_w, nb=bs, tm=seq_s)

    return (xp, xs,
            jnp.stack(outs_p[0]), jnp.stack(outs_p[1]), jnp.stack(outs_p[2]),
            jnp.stack(outs_s[0]), jnp.stack(outs_s[1]), jnp.stack(outs_s[2]))
```
